```python
import jax, jax.numpy as jnp
from jax import lax
import numpy as np

D_MODEL = 2048
BATCH = 1
SEQ = 16384
DEPTH = 4
DEC_BATCH = 8
DEC_SEQ = 64
PAST_LEN = 2048

CHUNK = 64
EPS = 1e-6
LRU_WIDTH = 2048
LRU_HEADS = 8
LRU_BLOCK = LRU_WIDTH // LRU_HEADS
LRU_CONV = 4
LRU_C = 8.0
CONV_WIDTH = 2048
SCONV_K = 3
PEER_HEADS = 8
PEER_DKEY = 256
PEER_DHALF = PEER_DKEY // 2
PEER_NKEYS = 128
PEER_NEXPERTS = PEER_NKEYS * PEER_NKEYS
PEER_TOPK = 16
PEER_BLOCK = 128
N_ADA = 6

kernel_name = 'griffin_shortconv_peer_stream_step'


def rms_norm(x, g):
    xf = x.astype(jnp.float32)
    y = xf * lax.rsqrt(jnp.mean(xf * xf, axis=-1, keepdims=True) + EPS)
    return (y * g.astype(jnp.float32)).astype(x.dtype)


def causal_dwconv(x, ctx, w):
    k_w = w.shape[0]
    s = x.shape[1]
    xp = jnp.concatenate([ctx.astype(x.dtype), x], axis=1)
    y = w[0] * xp[:, 0:s]
    for k in range(1, k_w):
        y = y + w[k] * xp[:, k:k + s]
    return y, xp[:, -(k_w - 1):]


def rg_lru(x, h0, pos, w_ga, b_ga, w_gx, b_gx, lam):
    b, s, w = x.shape
    xb = x.reshape(b, s, LRU_HEADS, LRU_BLOCK)
    r = jax.nn.sigmoid(jnp.einsum('bshi,hij->bshj', xb, w_ga).reshape(b, s, w) + b_ga)
    i = jax.nn.sigmoid(jnp.einsum('bshi,hij->bshj', xb, w_gx).reshape(b, s, w) + b_gx)
    log_a = (-LRU_C * r.astype(jnp.float32)) * jax.nn.softplus(-lam.astype(jnp.float32))
    a = jnp.exp(log_a)
    mult = jnp.sqrt(-jnp.expm1(2.0 * log_a))
    mult = jnp.where((pos == 0)[None, :, None], 1.0, mult)
    u = mult * (i * x).astype(jnp.float32)
    u = u.at[:, 0].add(a[:, 0] * h0.astype(jnp.float32))

    def comb(left, right):
        al, bl = left
        ar, br = right
        return (al * ar, ar * bl + br)

    _, h = lax.associative_scan(comb, (a, u), axis=1)
    return h.astype(x.dtype), h[:, -1].astype(x.dtype)


def peer(h, wq, keys, u, v):
    b, s, d = h.shape
    t_all = b * s
    n_blk = -(-t_all // PEER_BLOCK)
    pad = n_blk * PEER_BLOCK - t_all
    t = jnp.pad(h.reshape(t_all, d), ((0, pad), (0, 0))).reshape(n_blk, PEER_BLOCK, d)

    def one_block(tb):
        q = (tb @ wq).reshape(PEER_BLOCK, PEER_HEADS, 2, PEER_DHALF)
        sc = jnp.einsum('thpd,phnd->thpn', q, keys).astype(jnp.float32)
        top_s, top_i = lax.top_k(sc, PEER_TOPK)
        cand_s = top_s[:, :, 0, :, None] + top_s[:, :, 1, None, :]
        cand_i = top_i[:, :, 0, :, None] * PEER_NKEYS + top_i[:, :, 1, None, :]
        cand_s = cand_s.reshape(PEER_BLOCK, PEER_HEADS, PEER_TOPK * PEER_TOPK)
        cand_i = cand_i.reshape(PEER_BLOCK, PEER_HEADS, PEER_TOPK * PEER_TOPK)
        best_s, best_j = lax.top_k(cand_s, PEER_TOPK)
        idx = jnp.take_along_axis(cand_i, best_j, axis=-1)
        g = jax.nn.softmax(best_s, axis=-1).astype(tb.dtype)
        u_sel = u[idx]
        act = jax.nn.gelu(jnp.einsum('td,thkd->thk', tb, u_sel))
        v_sel = v[idx]
        return jnp.einsum('thk,thkd->td', g * act, v_sel)

    out = lax.map(one_block, t)
    return out.reshape(n_blk * PEER_BLOCK, d)[:t_all].reshape(b, s, d)


def layer(x, c, pos, lru_ctx, lru_h, sc_ctx, prm):
    mod = jnp.einsum('bd,de->be', jax.nn.silu(c), prm['ada_w']) + prm['ada_b']
    sh1, sc1, gt1, sh2, sc2, gt2 = jnp.split(mod[:, None, :], N_ADA, axis=-1)
    h = rms_norm(x, prm['norm1_g']) * (1 + sc1) + sh1
    proj = jnp.einsum('bsd,de->bse', h, prm['w_in'])
    cuts = np.cumsum([LRU_WIDTH, LRU_WIDTH, CONV_WIDTH, CONV_WIDTH, CONV_WIDTH, D_MODEL])
    xa, ya, bg, cg, vv, ga, gb = jnp.split(proj, [int(k) for k in cuts], axis=-1)
    xa_c, lru_ctx_new = causal_dwconv(xa, lru_ctx, prm['lru_conv_w'])
    xa_c = xa_c + prm['lru_conv_b']
    hseq, h_last = rg_lru(xa_c, lru_h, pos, prm['lru_ga_w'], prm['lru_ga_b'],
                          prm['lru_gx_w'], prm['lru_gx_b'], prm['lru_lambda'])
    y_a = hseq * jax.nn.gelu(ya)
    cv, sc_ctx_new = causal_dwconv(cg * vv, sc_ctx, prm['sconv_w'])
    y_b = bg * cv
    merged = (jax.nn.sigmoid(ga) * jnp.einsum('bsw,wd->bsd', y_a, prm['w_a_out'])
              + jax.nn.sigmoid(gb) * jnp.einsum('bsw,wd->bsd', y_b, prm['w_b_out']))
    x = x + gt1 * jnp.einsum('bsd,de->bse', merged, prm['w_o'])
    h2 = rms_norm(x, prm['norm2_g']) * (1 + sc2) + sh2
    x = x + gt2 * peer(h2, prm['peer_wq'], prm['peer_keys'], prm['peer_u'], prm['peer_v'])
    return x, lru_ctx_new, h_last, sc_ctx_new


def setup_inputs(seed: int = 0) -> dict:
    key = jax.random.key(seed)
    ks = jax.random.split(key, 32)
    f32 = jnp.float32
    nrm = lambda k, shp, sc: jax.random.normal(k, shp, f32) * sc
    n_in = 2 * LRU_WIDTH + 3 * CONV_WIDTH + 2 * D_MODEL
    a8 = jax.random.uniform(ks[17], (DEPTH, LRU_WIDTH), f32, minval=0.9, maxval=0.999)
    a_base = a8 ** (1.0 / LRU_C)
    lru_lambda = jnp.log(a_base) - jnp.log1p(-a_base)
    return {
        'x_prompt': nrm(ks[0], (BATCH, SEQ, D_MODEL), 1.0),
        'x_sample': nrm(ks[1], (DEC_BATCH, DEC_SEQ, D_MODEL), 1.0),
        'state_lru_conv': nrm(ks[2], (DEPTH, DEC_BATCH, LRU_CONV - 1, LRU_WIDTH), 1.0),
        'state_lru_h': nrm(ks[3], (DEPTH, DEC_BATCH, LRU_WIDTH), 0.5),
        'state_sconv': nrm(ks[4], (DEPTH, DEC_BATCH, SCONV_K - 1, CONV_WIDTH), 0.5),
        'c_prompt': nrm(ks[5], (BATCH, D_MODEL), 1.0),
        'c_sample': nrm(ks[6], (DEC_BATCH, D_MODEL), 1.0),
        'norm1_g': 1.0 + nrm(ks[7], (DEPTH, D_MODEL), 0.02),
        'norm2_g': 1.0 + nrm(ks[8], (DEPTH, D_MODEL), 0.02),
        'final_g': 1.0 + nrm(ks[9], (D_MODEL,), 0.02),
        'ada_w': nrm(ks[10], (DEPTH, D_MODEL, N_ADA * D_MODEL), 0.5 * D_MODEL ** -0.5),
        'ada_b': nrm(ks[11], (DEPTH, N_ADA * D_MODEL), 0.01),
        'w_in': nrm(ks[12], (DEPTH, D_MODEL, n_in), D_MODEL ** -0.5),
        'lru_conv_w': nrm(ks[13], (DEPTH, LRU_CONV, LRU_WIDTH), LRU_CONV ** -0.5),
        'lru_conv_b': nrm(ks[14], (DEPTH, LRU_WIDTH), 0.01),
        'lru_ga_w': nrm(ks[15], (DEPTH, LRU_HEADS, LRU_BLOCK, LRU_BLOCK), LRU_BLOCK ** -0.5),
        'lru_ga_b': nrm(ks[16], (DEPTH, LRU_WIDTH), 0.01),
        'lru_gx_w': nrm(ks[18], (DEPTH, LRU_HEADS, LRU_BLOCK, LRU_BLOCK), LRU_BLOCK ** -0.5),
        'lru_gx_b': nrm(ks[19], (DEPTH, LRU_WIDTH), 0.01),
        'lru_lambda': lru_lambda,
        'sconv_w': nrm(ks[20], (DEPTH, SCONV_K, CONV_WIDTH), SCONV_K ** -0.5),
        'w_a_out': nrm(ks[21], (DEPTH, LRU_WIDTH, D_MODEL), LRU_WIDTH ** -0.5),
        'w_b_out': nrm(ks[22], (DEPTH, CONV_WIDTH, D_MODEL), CONV_WIDTH ** -0.5),
        'w_o': nrm(ks[23], (DEPTH, D_MODEL, D_MODEL), D_MODEL ** -0.5),
        'peer_wq': nrm(ks[24], (DEPTH, D_MODEL, PEER_HEADS * PEER_DKEY), D_MODEL ** -0.5),
        'peer_keys': nrm(ks[25], (DEPTH, 2, PEER_HEADS, PEER_NKEYS, PEER_DHALF), PEER_DHALF ** -0.5),
        'peer_u': nrm(ks[26], (DEPTH, PEER_NEXPERTS, D_MODEL), D_MODEL ** -0.5),
        'peer_v': nrm(ks[27], (DEPTH, PEER_NEXPERTS, D_MODEL), PEER_HEADS ** -0.5),
    }


def reference(x_prompt, x_sample, state_lru_conv, state_lru_h, state_sconv, c_prompt, c_sample,
              norm1_g, norm2_g, final_g, ada_w, ada_b, w_in, lru_conv_w, lru_conv_b,
              lru_ga_w, lru_ga_b, lru_gx_w, lru_gx_b, lru_lambda, sconv_w,
              w_a_out, w_b_out, w_o, peer_wq, peer_keys, peer_u, peer_v):
    bp, sp, _ = x_prompt.shape
    pos_p = jnp.arange(sp, dtype=jnp.int32)
    pos_s = PAST_LEN + jnp.arange(x_sample.shape[1], dtype=jnp.int32)
    zero_lru_ctx = jnp.zeros((bp, LRU_CONV - 1, LRU_WIDTH), x_prompt.dtype)
    zero_h = jnp.zeros((bp, LRU_WIDTH), x_prompt.dtype)
    zero_sc_ctx = jnp.zeros((bp, SCONV_K - 1, CONV_WIDTH), x_prompt.dtype)
    xp, xs = x_prompt, x_sample
    lc_p, lh_p, sc_p, lc_s, lh_s, sc_s = [], [], [], [], [], []
    for l in range(DEPTH):
        prm = {
            'norm1_g': norm1_g[l], 'norm2_g': norm2_g[l], 'ada_w': ada_w[l], 'ada_b': ada_b[l],
            'w_in': w_in[l], 'lru_conv_w': lru_conv_w[l], 'lru_conv_b': lru_conv_b[l],
            'lru_ga_w': lru_ga_w[l], 'lru_ga_b': lru_ga_b[l], 'lru_gx_w': lru_gx_w[l],
            'lru_gx_b': lru_gx_b[l], 'lru_lambda': lru_lambda[l], 'sconv_w': sconv_w[l],
            'w_a_out': w_a_out[l], 'w_b_out': w_b_out[l], 'w_o': w_o[l],
            'peer_wq': peer_wq[l], 'peer_keys': peer_keys[l], 'peer_u': peer_u[l], 'peer_v': peer_v[l],
        }
        xp, a1, a2, a3 = layer(xp, c_prompt, pos_p, zero_lru_ctx, zero_h, zero_sc_ctx, prm)
        xs, b1, b2, b3 = layer(xs, c_sample, pos_s, state_lru_conv[l], state_lru_h[l], state_sconv[l], prm)
        lc_p.append(a1); lh_p.append(a2); sc_p.append(a3)
        lc_s.append(b1); lh_s.append(b2); sc_s.append(b3)
    y_prompt = rms_norm(xp, final_g)
    y_sample = rms_norm(xs, final_g)
    return (y_prompt, y_sample, jnp.stack(lc_p), jnp.stack(lh_p), jnp.stack(sc_p),
            jnp.stack(lc_s), jnp.stack(lh_s), jnp.stack(sc_s))
```

```python
import functools
import math

import jax
import jax.numpy as jnp
from jax import lax
from jax.experimental import pallas as pl
from jax.experimental.pallas import tpu as pltpu

EPS = 1e-6
LRU_C = 8.0
LRU_BLOCK = 256
PEER_HEADS = 8
PEER_DHALF = 128
PEER_NKEYS = 128
PEER_TOPK = 16
N_ADA = 6
PAST_LEN = 2048
DEC_BATCH = 8
DEC_SEQ = 64

TB = 512
MOD_ROWS = 8
LANES = 128
SUBLANES = 8
VMEM_LIMIT = 56 * 1024 * 1024

F32 = jnp.float32
BF16 = jnp.bfloat16


def _params(sem):
    return pltpu.CompilerParams(dimension_semantics=sem, vmem_limit_bytes=VMEM_LIMIT)


def _gelu(x):
    c = math.sqrt(2.0 / math.pi)
    return x * (0.5 * (1.0 + jnp.tanh(c * (x + 0.044715 * (x * x * x)))))


def _rms_mod(x, g, sc, sh):
    d = x.shape[-1]
    ms = jnp.mean(x * x, axis=-1, keepdims=True)
    y = (x * lax.rsqrt(ms + EPS)) * g
    y3 = y.reshape(MOD_ROWS, TB // MOD_ROWS, d)
    y3 = y3 * (1.0 + sc)[:, None, :] + sh[:, None, :]
    return y3.reshape(TB, d)


def _gate_rows(v, gt):
    d = v.shape[-1]
    return (v.reshape(MOD_ROWS, TB // MOD_ROWS, d) * gt[:, None, :]).reshape(TB, d)


def _ada_kernel(c_ref, w_ref, b_ref, o_ref):
    c = c_ref[...]
    s = (c * jax.nn.sigmoid(c)).astype(BF16)
    o_ref[...] = jnp.dot(s, w_ref[...].astype(BF16), preferred_element_type=F32) + b_ref[...]


def _ada(c16, ada_w, ada_b):
    depth, d, n = ada_w.shape
    tn = 1024
    return pl.pallas_call(
        _ada_kernel,
        grid=(depth, n // tn),
        in_specs=[pl.BlockSpec((2 * MOD_ROWS, d), lambda l, j: (0, 0)),
                  pl.BlockSpec((None, d, tn), lambda l, j: (l, 0, j)),
                  pl.BlockSpec((None, 1, tn), lambda l, j: (l, 0, j))],
        out_specs=pl.BlockSpec((None, 2 * MOD_ROWS, tn), lambda l, j: (l, 0, j)),
        out_shape=jax.ShapeDtypeStruct((depth, 2 * MOD_ROWS, n), F32),
        compiler_params=_params(("arbitrary", "arbitrary")),
        name="ada_mod",
    )(c16, ada_w, ada_b.reshape(depth, 1, n))


def _compose_kernel(k_ref, wq_ref, o_ref):
    o_ref[...] = lax.dot_general(
        k_ref[...], wq_ref[...], (((1,), (1,)), ((), ())),
        precision=lax.Precision.HIGHEST, preferred_element_type=F32).astype(BF16)


def _compose(peer_keys, peer_wq):
    depth, d, nq = peer_wq.shape
    nb = 2 * PEER_HEADS
    return pl.pallas_call(
        _compose_kernel,
        grid=(depth, nb),
        in_specs=[pl.BlockSpec((None, None, None, PEER_NKEYS, PEER_DHALF),
                               lambda l, b: (l, b // PEER_HEADS, b % PEER_HEADS, 0, 0)),
                  pl.BlockSpec((None, d, PEER_DHALF),
                               lambda l, b: (l, 0, (b % PEER_HEADS) * 2 + b // PEER_HEADS))],
        out_specs=pl.BlockSpec((None, PEER_NKEYS, d), lambda l, b: (l, b, 0)),
        out_shape=jax.ShapeDtypeStruct((depth, nb * PEER_NKEYS, d), BF16),
        compiler_params=_params(("arbitrary", "arbitrary")),
        name="peer_compose",
    )(peer_keys, peer_wq)


def _inproj_kernel(has_delta, *refs):
    if has_delta:
        x_ref, po_ref, gt_ref, g_ref, sc_ref, sh_ref, w_ref, proj_ref, xo_ref, h_scr = refs
    else:
        x_ref, g_ref, sc_ref, sh_ref, w_ref, proj_ref, h_scr = refs

    @pl.when(pl.program_id(1) == 0)
    def _():
        x = x_ref[...]
        if has_delta:
            x = x + _gate_rows(po_ref[...], gt_ref[...])
            xo_ref[...] = x
        h_scr[...] = _rms_mod(x, g_ref[...], sc_ref[...], sh_ref[...]).astype(BF16)

    proj_ref[...] = jnp.dot(h_scr[...], w_ref[...], preferred_element_type=F32)


def _mod_spec(l, k, d, n_prompt_tiles):
    return pl.BlockSpec((None, MOD_ROWS, d),
                        lambda i, *_: (l, jnp.where(i < n_prompt_tiles, 0, 1), k))


def _inproj(l, x, po, mod, norm_g, w_in, n_prompt_tiles):
    t, d = x.shape
    n = w_in.shape[-1]
    tn = 1024
    has_delta = po is not None
    row = pl.BlockSpec((TB, d), lambda i, j: (i, 0))
    in_specs = [row]
    args = [x]
    if has_delta:
        in_specs += [row, _mod_spec(l - 1, 5, d, n_prompt_tiles)]
        args += [po, mod]
    in_specs += [pl.BlockSpec((None, 1, d), lambda i, j: (l, 0, 0)),
                 _mod_spec(l, 1, d, n_prompt_tiles),
                 _mod_spec(l, 0, d, n_prompt_tiles),
                 pl.BlockSpec((None, d, tn), lambda i, j: (l, 0, j))]
    args += [norm_g, mod, mod, w_in]
    out_specs = [pl.BlockSpec((TB, tn), lambda i, j: (i, j))]
    out_shape = [jax.ShapeDtypeStruct((t, n), F32)]
    if has_delta:
        out_specs.append(row)
        out_shape.append(jax.ShapeDtypeStruct((t, d), F32))
    res = pl.pallas_call(
        functools.partial(_inproj_kernel, has_delta),
        grid=(t // TB, n // tn),
        in_specs=in_specs, out_specs=out_specs, out_shape=out_shape,
        scratch_shapes=[pltpu.VMEM((TB, d), BF16)],
        compiler_params=_params(("arbitrary", "arbitrary")),
        name="in_proj",
    )(*args)
    if has_delta:
        return res[0], res[1]
    return res[0], x


def _mix_kernel(tt_rows, pos0, aliased, *refs):
    (xa_ref, ya_ref, bg_ref, cg_ref, vv_ref, sconv_ref, sh_ref, ssc_ref,
     cw_ref, cb_ref, gaw_ref, gab_ref, gxw_ref, gxb_ref, lam_ref, sw_ref) = refs[:16]
    refs = refs[16 + (2 if aliased else 0):]
    yao_ref, ybo_ref, nconv_ref, nh_ref, nsc_ref, xbuf, cvbuf, hcar = refs
    tt = pl.program_id(2)
    n = tt_rows

    @pl.when(tt == 0)
    def _():
        xbuf[5:8, :] = sconv_ref[...]
        cvbuf[6:8, :] = ssc_ref[...]
        hcar[...] = sh_ref[...]

    xa = xa_ref[...]
    xbuf[8:8 + n, :] = xa
    cw = cw_ref[...]
    xc = cw[0:1] * xbuf[5:5 + n, :]
    xc = xc + cw[1:2] * xbuf[6:6 + n, :]
    xc = xc + cw[2:3] * xbuf[7:7 + n, :]
    xc = xc + cw[3:4] * xa
    xc = xc + cb_ref[...]

    xcb = xc.astype(BF16)
    r = jax.nn.sigmoid(jnp.dot(xcb, gaw_ref[...], preferred_element_type=F32) + gab_ref[...])
    ig = jax.nn.sigmoid(jnp.dot(xcb, gxw_ref[...], preferred_element_type=F32) + gxb_ref[...])
    z = -lam_ref[...]
    softplus = jnp.maximum(z, 0.0) + jnp.log1p(jnp.exp(-jnp.abs(z)))
    log_a = (-LRU_C * r) * softplus
    a = jnp.exp(log_a)
    mult = jnp.sqrt(-jnp.tanh(log_a) * (a * a + 1.0))
    row = lax.broadcasted_iota(jnp.int32, (n, 1), 0)
    mult = jnp.where(row + (pos0 + tt * n) == 0, 1.0, mult)
    u = mult * (ig * xc)

    acc_a, acc_u = a, u
    s = 1
    while s < n:
        keep = row >= s
        prev_u = jnp.where(keep, pltpu.roll(acc_u, s, 0), 0.0)
        prev_a = jnp.where(keep, pltpu.roll(acc_a, s, 0), 1.0)
        acc_u = acc_a * prev_u + acc_u
        acc_a = acc_a * prev_a
        s *= 2
    h = acc_u + acc_a * hcar[...]
    yao_ref[...] = (h * _gelu(ya_ref[...])).astype(BF16)

    cv_in = cg_ref[...] * vv_ref[...]
    cvbuf[8:8 + n, :] = cv_in
    sw = sw_ref[...]
    cv = sw[0:1] * cvbuf[6:6 + n, :]
    cv = cv + sw[1:2] * cvbuf[7:7 + n, :]
    cv = cv + sw[2:3] * cv_in
    ybo_ref[...] = (bg_ref[...] * cv).astype(BF16)

    xbuf[5:8, :] = xa[n - 3:n, :]
    cvbuf[6:8, :] = cv_in[n - 2:n, :]
    hcar[...] = h[n - 1:n, :]

    @pl.when(tt == pl.num_programs(2) - 1)
    def _():
        nconv_ref[...] = xa[n - 3:n, :]
        nh_ref[...] = h[n - 1:n, :]
        nsc_ref[...] = cv_in[n - 2:n, :]


def _mix(l, proj, states, prm, tt_rows, n_streams, n_tiles, row0, pos0, y_prev, width):
    t = proj.shape[0]
    cb = LRU_BLOCK
    nblk = width // cb
    rb0 = row0 // tt_rows
    st_conv, st_h, st_sc = states
    aliased = y_prev is not None

    def col(k):
        return pl.BlockSpec((tt_rows, cb),
                            lambda hb, b, tt: (rb0 + b * n_tiles + tt, k * nblk + hb))

    def vec(rows):
        return pl.BlockSpec((None, rows, cb), lambda hb, b, tt: (l, 0, hb))

    def st(rows):
        return pl.BlockSpec((None, rows, cb), lambda hb, b, tt: (b, 0, hb))

    gate_w = pl.BlockSpec((None, None, cb, cb), lambda hb, b, tt: (l, hb, 0, 0))
    in_specs = [col(0), col(1), col(2), col(3), col(4), st(3), st(1), st(2),
                vec(4), vec(1), gate_w, vec(1), gate_w, vec(1), vec(1), vec(3)]
    args = [proj, proj, proj, proj, proj, st_conv, st_h, st_sc,
            prm['lru_conv_w'], prm['lru_conv_b'], prm['lru_ga_w'], prm['lru_ga_b'],
            prm['lru_gx_w'], prm['lru_gx_b'], prm['lru_lambda'], prm['sconv_w']]
    io_alias = {}
    if aliased:
        in_specs += [pl.BlockSpec(memory_space=pl.ANY)] * 2
        args += list(y_prev)
        io_alias = {16: 0, 17: 1}
    yspec = pl.BlockSpec((tt_rows, cb), lambda hb, b, tt: (rb0 + b * n_tiles + tt, hb))
    out_specs = [yspec, yspec, st(3), st(1), st(2)]
    out_shape = [jax.ShapeDtypeStruct((t, width), BF16),
                 jax.ShapeDtypeStruct((t, width), BF16),
                 jax.ShapeDtypeStruct((n_streams, 3, width), F32),
                 jax.ShapeDtypeStruct((n_streams, 1, width), F32),
                 jax.ShapeDtypeStruct((n_streams, 2, width), F32)]
    return pl.pallas_call(
        functools.partial(_mix_kernel, tt_rows, pos0, aliased),
        grid=(nblk, n_streams, n_tiles),
        in_specs=in_specs, out_specs=out_specs, out_shape=out_shape,
        scratch_shapes=[pltpu.VMEM((SUBLANES + tt_rows, cb), F32),
                        pltpu.VMEM((SUBLANES + tt_rows, cb), F32),
                        pltpu.VMEM((1, cb), F32)],
        input_output_aliases=io_alias,
        compiler_params=_params(("arbitrary", "arbitrary", "arbitrary")),
        name="mix_sample" if aliased else "mix_prompt",
    )(*args)


def _merge_kernel(ya_ref, yb_ref, ga_ref, gb_ref, wa_ref, wb_ref, o_ref):
    pa = jnp.dot(ya_ref[...], wa_ref[...], preferred_element_type=F32)
    pb = jnp.dot(yb_ref[...], wb_ref[...], preferred_element_type=F32)
    o_ref[...] = (jax.nn.sigmoid(ga_ref[...]) * pa + jax.nn.sigmoid(gb_ref[...]) * pb).astype(BF16)


def _merge(l, y_a, y_b, proj, w_a, w_b, gate_col0):
    t, w = y_a.shape
    d = w_a.shape[-1]
    tn = 1024
    nj = d // tn
    gc = gate_col0 // tn
    yspec = pl.BlockSpec((TB, w), lambda i, j: (i, 0))
    wspec = pl.BlockSpec((None, w, tn), lambda i, j: (l, 0, j))
    return pl.pallas_call(
        _merge_kernel,
        grid=(t // TB, nj),
        in_specs=[yspec, yspec,
                  pl.BlockSpec((TB, tn), lambda i, j: (i, gc + j)),
                  pl.BlockSpec((TB, tn), lambda i, j: (i, gc + nj + j)),
                  wspec, wspec],
        out_specs=pl.BlockSpec((TB, tn), lambda i, j: (i, j)),
        out_shape=jax.ShapeDtypeStruct((t, d), BF16),
        compiler_params=_params(("arbitrary", "arbitrary")),
        name="merge",
    )(y_a, y_b, proj, proj, w_a, w_b)


def _outproj_kernel(m_ref, w_ref, x_ref, gt_ref, o_ref):
    y = jnp.dot(m_ref[...], w_ref[...], preferred_element_type=F32)
    o_ref[...] = x_ref[...] + _gate_rows(y, gt_ref[...])


def _outproj(l, merged, w_o, x, mod, n_prompt_tiles):
    t, d = x.shape
    row = pl.BlockSpec((TB, d), lambda i: (i, 0))
    return pl.pallas_call(
        _outproj_kernel,
        grid=(t // TB,),
        in_specs=[row, pl.BlockSpec((None, d, d), lambda i: (l, 0, 0)), row,
                  _mod_spec(l, 2, d, n_prompt_tiles)],
        out_specs=row,
        out_shape=jax.ShapeDtypeStruct((t, d), F32),
        compiler_params=_params(("arbitrary",)),
        name="out_proj",
    )(merged, w_o, x, mod)


def _cmpx(v, i, j):
    hi = jnp.maximum(v[i], v[j])
    lo = jnp.minimum(v[i], v[j])
    v[i], v[j] = hi, lo


def _bitonic_merge_desc(v):
    n = len(v)
    j = n // 2
    while j >= 1:
        for i in range(n):
            p = i ^ j
            if p > i:
                _cmpx(v, i, p)
        j //= 2


def _bitonic_sort_desc(v):
    n = len(v)
    k = 2
    while k <= n:
        j = k // 2
        while j >= 1:
            for i in range(n):
                p = i ^ j
                if p > i:
                    if (i & k) == 0 or k == n:
                        _cmpx(v, i, p)
                    else:
                        _cmpx(v, p, i)
            j //= 2
        k *= 2


def _merge_top(a, b):
    n = len(a)
    c = [jnp.maximum(a[i], b[n - 1 - i]) for i in range(n)]
    _bitonic_merge_desc(c)
    return c


def _top16_of_keys(rows):
    v = list(rows)
    _bitonic_sort_desc(v)
    for shift in (4, 2, 1):
        other = [pltpu.roll(x, shift, 0) for x in v]
        v = _merge_top(v, other)
    return v


def _query_kernel(x_ref, g_ref, sc_ref, sh_ref, mc_ref,
                  h2t_ref, e1_ref, c_ref, e2_ref, s2_ref, st_scr):
    nk = PEER_NKEYS
    k = PEER_TOPK
    h = _rms_mod(x_ref[...], g_ref[...], sc_ref[...], sh_ref[...])
    ht = h.T.astype(BF16)
    h2t_ref[...] = ht
    st_scr[...] = jnp.dot(mc_ref[...], ht, preferred_element_type=F32)
    neg = jnp.full((SUBLANES, LANES), -jnp.inf, F32)

    for lb in range(TB // LANES):
        lanes = slice(lb * LANES, (lb + 1) * LANES)

        def body(hh, carry):
            r1 = pl.multiple_of(hh * nk, nk)
            r2 = pl.multiple_of((PEER_HEADS + hh) * nk, nk)
            s1 = st_scr[pl.ds(r1, nk), lanes]
            s2 = st_scr[pl.ds(r2, nk), lanes]
            s1l = [s1[SUBLANES * i:SUBLANES * (i + 1), :] for i in range(nk // SUBLANES)]
            s2l = [s2[SUBLANES * i:SUBLANES * (i + 1), :] for i in range(nk // SUBLANES)]
            m1 = _top16_of_keys(s1l)
            m2 = _top16_of_keys(s2l)
            g0 = [m1[0] + m2[b] for b in range(k)]
            g1 = [m1[a] + m2[0] for a in range(1, k)] + [neg]
            rest = [m1[a] + m2[b] for a in range(1, k) for b in range(1, k)
                    if (a + 1) * (b + 1) <= k]
            rest = rest + [neg] * (2 * k - len(rest))
            g2, g3 = rest[:k], rest[k:]
            _bitonic_sort_desc(g2)
            _bitonic_sort_desc(g3)
            top = _merge_top(_merge_top(g0, g1), _merge_top(g2, g3))
            thr = top[k - 1]
            zsum = jnp.exp(top[0] - top[0])
            for i in range(1, k):
                zsum = zsum + jnp.exp(top[i] - top[0])
            inv_z = 1.0 / zsum
            e1_ref[pl.ds(r1, nk), lanes] = jnp.concatenate(
                [jnp.exp(x - m1[0]) for x in s1l], axis=0)
            c_ref[pl.ds(r1, nk), lanes] = jnp.concatenate([thr - x for x in s1l], axis=0)
            e2_ref[pl.ds(r1, nk), lanes] = jnp.concatenate(
                [jnp.exp(x - m2[0]) * inv_z for x in s2l], axis=0)
            s2_ref[pl.ds(r1, nk), lanes] = s2
            return carry

        lax.fori_loop(0, PEER_HEADS, body, 0)


def _query(l, x, mod, norm_g, mc, n_prompt_tiles):
    t, d = x.shape
    nq = mc.shape[1]
    rows = PEER_HEADS * PEER_NKEYS
    colblk = pl.BlockSpec((rows, TB), lambda i: (0, i))
    sds = jax.ShapeDtypeStruct((rows, t), F32)
    return pl.pallas_call(
        _query_kernel,
        grid=(t // TB,),
        in_specs=[pl.BlockSpec((TB, d), lambda i: (i, 0)),
                  pl.BlockSpec((None, 1, d), lambda i: (l, 0, 0)),
                  _mod_spec(l, 4, d, n_prompt_tiles),
                  _mod_spec(l, 3, d, n_prompt_tiles),
                  pl.BlockSpec((None, nq, d), lambda i: (l, 0, 0))],
        out_specs=[pl.BlockSpec((d, TB), lambda i: (0, i)), colblk, colblk, colblk, colblk],
        out_shape=[jax.ShapeDtypeStruct((d, t), BF16), sds, sds, sds, sds],
        scratch_shapes=[pltpu.VMEM((nq, TB), F32)],
        compiler_params=_params(("arbitrary",)),
        name="peer_query",
    )(x, norm_g, mod, mod, mc)


def _peer_kernel(n_i, h2t_ref, u_ref, vt_ref, e1_ref, c_ref, e2_ref, s2_ref, po_ref,
                 acc_scr, w_scr):
    et = pl.program_id(1)
    nk = PEER_NKEYS

    @pl.when(et == 0)
    def _():
        acc_scr[...] = jnp.zeros_like(acc_scr)

    for il in range(n_i):
        for lb in range(TB // LANES):
            lanes = slice(lb * LANES, (lb + 1) * LANES)
            acc = jnp.zeros((nk, LANES), F32)
            for hh in range(PEER_HEADS):
                cr = c_ref[hh, il:il + 1, lanes]
                er = e1_ref[hh, il:il + 1, lanes]
                acc = acc + jnp.where(s2_ref[hh, :, lanes] >= cr, e2_ref[hh, :, lanes] * er, 0.0)
            w_scr[il * nk:(il + 1) * nk, lanes] = acc

    act = jnp.dot(u_ref[...], h2t_ref[...], preferred_element_type=F32)
    y = (w_scr[...] * _gelu(act)).astype(BF16)
    acc_scr[...] += jnp.dot(vt_ref[...], y, preferred_element_type=F32)

    @pl.when(et == pl.num_programs(1) - 1)
    def _():
        po_ref[...] = acc_scr[...].T


def _peer(l, h2t, u, vt, e1, c, e2, s2):
    d, t = h2t.shape
    ne = u.shape[1]
    n_i = 8
    te = n_i * PEER_NKEYS
    small = pl.BlockSpec((PEER_HEADS, n_i, TB), lambda i, e: (0, e, i))
    full = pl.BlockSpec((PEER_HEADS, PEER_NKEYS, TB), lambda i, e: (0, 0, i))
    return pl.pallas_call(
        functools.partial(_peer_kernel, n_i),
        grid=(t // TB, ne // te),
        in_specs=[pl.BlockSpec((d, TB), lambda i, e: (0, i)),
                  pl.BlockSpec((None, te, d), lambda i, e: (l, e, 0)),
                  pl.BlockSpec((None, d, te), lambda i, e: (l, 0, e)),
                  small, small, full, full],
        out_specs=pl.BlockSpec((TB, d), lambda i, e: (i, 0)),
        out_shape=jax.ShapeDtypeStruct((t, d), F32),
        scratch_shapes=[pltpu.VMEM((d, TB), F32), pltpu.VMEM((te, TB), F32)],
        compiler_params=_params(("arbitrary", "arbitrary")),
        name="peer_dense",
    )(h2t, u, vt, e1, c, e2, s2)


def _final_kernel(n_prompt_tiles, x_ref, po_ref, gt_ref, g_ref, yp_ref, ys_ref):
    x = x_ref[...] + _gate_rows(po_ref[...], gt_ref[...])
    ms = jnp.mean(x * x, axis=-1, keepdims=True)
    y = (x * lax.rsqrt(ms + EPS)) * g_ref[...]
    i = pl.program_id(0)

    @pl.when(i < n_prompt_tiles)
    def _():
        yp_ref[...] = y

    @pl.when(i >= n_prompt_tiles)
    def _():
        ys_ref[...] = y


def _final(l, x, po, mod, final_g, n_prompt_tiles):
    t, d = x.shape
    row = pl.BlockSpec((TB, d), lambda i: (i, 0))
    return pl.pallas_call(
        functools.partial(_final_kernel, n_prompt_tiles),
        grid=(t // TB,),
        in_specs=[row, row, _mod_spec(l, 5, d, n_prompt_tiles),
                  pl.BlockSpec((1, d), lambda i: (0, 0))],
        out_specs=[pl.BlockSpec((TB, d), lambda i: (jnp.minimum(i, n_prompt_tiles - 1), 0)),
                   pl.BlockSpec((TB, d), lambda i: (0, 0))],
        out_shape=[jax.ShapeDtypeStruct((n_prompt_tiles * TB, d), F32),
                   jax.ShapeDtypeStruct((TB, d), F32)],
        compiler_params=_params(("arbitrary",)),
        name="final_norm",
    )(x, po, mod, final_g.reshape(1, d))


def kernel(x_prompt, x_sample, state_lru_conv, state_lru_h, state_sconv, c_prompt, c_sample,
           norm1_g, norm2_g, final_g, ada_w, ada_b, w_in, lru_conv_w, lru_conv_b,
           lru_ga_w, lru_ga_b, lru_gx_w, lru_gx_b, lru_lambda, sconv_w,
           w_a_out, w_b_out, w_o, peer_wq, peer_keys, peer_u, peer_v):
    bp, sp, d = x_prompt.shape
    bs, ss, _ = x_sample.shape
    depth = w_in.shape[0]
    width = lru_lambda.shape[-1]
    assert bp == 1 and bs == DEC_BATCH and ss == DEC_SEQ and bs * ss == TB and sp % TB == 0
    n_prompt_tiles = sp // TB

    x = jnp.concatenate([x_prompt.reshape(sp, d), x_sample.reshape(bs * ss, d)], axis=0)
    c16 = jnp.concatenate([jnp.broadcast_to(c_prompt, (MOD_ROWS, d)), c_sample], axis=0)
    mod = _ada(c16, ada_w, ada_b)
    mc = _compose(peer_keys, peer_wq)

    w_in_b = w_in.astype(BF16)
    w_a_b = w_a_out.astype(BF16)
    w_b_b = w_b_out.astype(BF16)
    w_o_b = w_o.astype(BF16)
    u_b = peer_u.astype(BF16)
    vt_b = jnp.swapaxes(peer_v, 1, 2).astype(BF16)
    prm = {
        'lru_conv_w': lru_conv_w, 'lru_conv_b': lru_conv_b.reshape(depth, 1, width),
        'lru_ga_w': lru_ga_w.astype(BF16), 'lru_ga_b': lru_ga_b.reshape(depth, 1, width),
        'lru_gx_w': lru_gx_w.astype(BF16), 'lru_gx_b': lru_gx_b.reshape(depth, 1, width),
        'lru_lambda': lru_lambda.reshape(depth, 1, width), 'sconv_w': sconv_w,
    }
    n1 = norm1_g.reshape(depth, 1, d)
    n2 = norm2_g.reshape(depth, 1, d)
    zero_states = (jnp.zeros((1, 3, width), F32), jnp.zeros((1, 1, width), F32),
                   jnp.zeros((1, 2, width), F32))

    outs_p, outs_s = [], []
    po = None
    for l in range(depth):
        proj, x = _inproj(l, x, po, mod, n1, w_in_b, n_prompt_tiles)
        y_a, y_b, pc, ph, psc = _mix(l, proj, zero_states, prm, TB, 1, n_prompt_tiles,
                                     0, 0, None, width)
        s_states = (state_lru_conv[l], state_lru_h[l].reshape(bs, 1, width), state_sconv[l])
        y_a, y_b, sc_, sh_, ssc_ = _mix(l, proj, s_states, prm, ss, bs, 1,
                                        sp, PAST_LEN, (y_a, y_b), width)
        outs_p.append((pc, ph.reshape(1, width), psc))
        outs_s.append((sc_, sh_.reshape(bs, width), ssc_))
        merged = _merge(l, y_a, y_b, proj, w_a_b, w_b_b, 2 * width + 3 * width)
        x = _outproj(l, merged, w_o_b, x, mod, n_prompt_tiles)
        h2t, e1, c, e2, s2 = _query(l, x, mod, n2, mc, n_prompt_tiles)
        shape3 = (PEER_HEADS, PEER_NKEYS, x.shape[0])
        po = _peer(l, h2t, u_b, vt_b, e1.reshape(shape3), c.reshape(shape3),
                   e2.reshape(shape3), s2.reshape(shape3))
    y_p, y_s = _final(depth - 1, x, po, mod, final_g, n_prompt_tiles)

    return (y_p.reshape(bp, sp, d), y_s.reshape(bs, ss, d),
            jnp.stack([o[0] for o in outs_p]), jnp.stack([o[1] for o in outs_p]),
            jnp.stack([o[2] for o in outs_p]),
            jnp.stack([o[0] for o in outs_s]), jnp.stack([o[1] for o in outs_s]),
            jnp.stack([o[2] for o in outs_s]))
```

```python
import functools
import math

import jax
import jax.numpy as jnp
from jax import lax
from jax.experimental import pallas as pl
from jax.experimental.pallas import tpu as pltpu

EPS = 1e-6
LRU_C = 8.0
LRU_BLOCK = 256
PEER_HEADS = 8
PEER_DHALF = 128
PEER_NKEYS = 128
PEER_TOPK = 16
N_ADA = 6
PAST_LEN = 2048
DEC_BATCH = 8
DEC_SEQ = 64

TB = 512
MOD_ROWS = 8
LANES = 128
SUBLANES = 8
VMEM_LIMIT = 56 * 1024 * 1024

F32 = jnp.float32
BF16 = jnp.bfloat16


def _params(sem):
    return pltpu.CompilerParams(dimension_semantics=sem, vmem_limit_bytes=VMEM_LIMIT)


def _gelu(x):
    c = math.sqrt(2.0 / math.pi)
    return x * (0.5 * (1.0 + jnp.tanh(c * (x + 0.044715 * (x * x * x)))))


def _rms_mod(x, g, sc, sh):
    d = x.shape[-1]
    ms = jnp.mean(x * x, axis=-1, keepdims=True)
    y = (x * lax.rsqrt(ms + EPS)) * g
    y3 = y.reshape(MOD_ROWS, TB // MOD_ROWS, d)
    y3 = y3 * (1.0 + sc)[:, None, :] + sh[:, None, :]
    return y3.reshape(TB, d)


def _gate_rows(v, gt):
    d = v.shape[-1]
    return (v.reshape(MOD_ROWS, TB // MOD_ROWS, d) * gt[:, None, :]).reshape(TB, d)


def _ada_kernel(c_ref, w_ref, b_ref, o_ref):
    c = c_ref[...]
    s = (c * jax.nn.sigmoid(c)).astype(BF16)
    o_ref[...] = jnp.dot(s, w_ref[...].astype(BF16), preferred_element_type=F32) + b_ref[...]


def _ada(c16, ada_w, ada_b):
    depth, d, n = ada_w.shape
    tn = 1024
    return pl.pallas_call(
        _ada_kernel,
        grid=(depth, n // tn),
        in_specs=[pl.BlockSpec((2 * MOD_ROWS, d), lambda l, j: (0, 0)),
                  pl.BlockSpec((None, d, tn), lambda l, j: (l, 0, j)),
                  pl.BlockSpec((None, 1, tn), lambda l, j: (l, 0, j))],
        out_specs=pl.BlockSpec((None, 2 * MOD_ROWS, tn), lambda l, j: (l, 0, j)),
        out_shape=jax.ShapeDtypeStruct((depth, 2 * MOD_ROWS, n), F32),
        compiler_params=_params(("arbitrary", "arbitrary")),
        name="ada_mod",
    )(c16, ada_w, ada_b.reshape(depth, 1, n))


def _compose_kernel(k_ref, wq_ref, o_ref):
    o_ref[...] = lax.dot_general(
        k_ref[...], wq_ref[...], (((1,), (1,)), ((), ())),
        precision=lax.Precision.HIGHEST, preferred_element_type=F32).astype(BF16)


def _compose(peer_keys, peer_wq):
    depth, d, nq = peer_wq.shape
    nb = 2 * PEER_HEADS
    return pl.pallas_call(
        _compose_kernel,
        grid=(depth, nb),
        in_specs=[pl.BlockSpec((None, None, None, PEER_NKEYS, PEER_DHALF),
                               lambda l, b: (l, b // PEER_HEADS, b % PEER_HEADS, 0, 0)),
                  pl.BlockSpec((None, d, PEER_DHALF),
                               lambda l, b: (l, 0, (b % PEER_HEADS) * 2 + b // PEER_HEADS))],
        out_specs=pl.BlockSpec((None, PEER_NKEYS, d), lambda l, b: (l, b, 0)),
        out_shape=jax.ShapeDtypeStruct((depth, nb * PEER_NKEYS, d), BF16),
        compiler_params=_params(("arbitrary", "arbitrary")),
        name="peer_compose",
    )(peer_keys, peer_wq)


def _norm_kernel(has_delta, *refs):
    if has_delta:
        x_ref, po_ref, gt_ref, g_ref, sc_ref, sh_ref, h_ref, xo_ref = refs
    else:
        x_ref, g_ref, sc_ref, sh_ref, h_ref = refs
    x = x_ref[...]
    if has_delta:
        x = x + _gate_rows(po_ref[...], gt_ref[...])
        xo_ref[...] = x
    h_ref[...] = _rms_mod(x, g_ref[...], sc_ref[...], sh_ref[...]).astype(BF16)


def _mod_spec(l, k, d, n_prompt_tiles):
    return pl.BlockSpec((None, MOD_ROWS, d),
                        lambda i, *_: (l, jnp.where(i < n_prompt_tiles, 0, 1), k))


def _norm1(l, x, po, mod, norm_g, n_prompt_tiles):
    t, d = x.shape
    has_delta = po is not None
    row = pl.BlockSpec((TB, d), lambda i: (i, 0))
    in_specs = [row]
    args = [x]
    if has_delta:
        in_specs += [row, _mod_spec(l - 1, 5, d, n_prompt_tiles)]
        args += [po, mod]
    in_specs += [pl.BlockSpec((None, 1, d), lambda i: (l, 0, 0)),
                 _mod_spec(l, 1, d, n_prompt_tiles),
                 _mod_spec(l, 0, d, n_prompt_tiles)]
    args += [norm_g, mod, mod]
    out_specs = [row]
    out_shape = [jax.ShapeDtypeStruct((t, d), BF16)]
    if has_delta:
        out_specs.append(row)
        out_shape.append(jax.ShapeDtypeStruct((t, d), F32))
    res = pl.pallas_call(
        functools.partial(_norm_kernel, has_delta),
        grid=(t // TB,),
        in_specs=in_specs, out_specs=out_specs, out_shape=out_shape,
        compiler_params=_params(("arbitrary",)),
        name="norm1",
    )(*args)
    if has_delta:
        return res[0], res[1]
    return res[0], x


def _inproj_kernel(h_ref, w_ref, o_ref):
    o_ref[...] = jnp.dot(h_ref[...], w_ref[...], preferred_element_type=F32)


def _inproj(l, h, w_in):
    t, d = h.shape
    n = w_in.shape[-1]
    tn = 2048
    return pl.pallas_call(
        _inproj_kernel,
        grid=(n // tn, t // TB),
        in_specs=[pl.BlockSpec((TB, d), lambda j, i: (i, 0)),
                  pl.BlockSpec((None, d, tn), lambda j, i: (l, 0, j))],
        out_specs=pl.BlockSpec((TB, tn), lambda j, i: (i, j)),
        out_shape=jax.ShapeDtypeStruct((t, n), F32),
        compiler_params=_params(("arbitrary", "arbitrary")),
        name="in_proj",
    )(h, w_in)


def _mix_kernel(tt_rows, pos0, aliased, *refs):
    (xa_ref, ya_ref, bg_ref, cg_ref, vv_ref, sconv_ref, sh_ref, ssc_ref,
     cw_ref, cb_ref, gaw_ref, gab_ref, gxw_ref, gxb_ref, lam_ref, sw_ref) = refs[:16]
    refs = refs[16 + (2 if aliased else 0):]
    yao_ref, ybo_ref, nconv_ref, nh_ref, nsc_ref, xbuf, cvbuf, hcar = refs
    tt = pl.program_id(2)
    n = tt_rows

    @pl.when(tt == 0)
    def _():
        xbuf[5:8, :] = sconv_ref[...]
        cvbuf[6:8, :] = ssc_ref[...]
        hcar[...] = sh_ref[...]

    xa = xa_ref[...]
    xbuf[8:8 + n, :] = xa
    cw = cw_ref[...]
    xc = cw[0:1] * xbuf[5:5 + n, :]
    xc = xc + cw[1:2] * xbuf[6:6 + n, :]
    xc = xc + cw[2:3] * xbuf[7:7 + n, :]
    xc = xc + cw[3:4] * xa
    xc = xc + cb_ref[...]

    xcb = xc.astype(BF16)
    r = jax.nn.sigmoid(jnp.dot(xcb, gaw_ref[...], preferred_element_type=F32) + gab_ref[...])
    ig = jax.nn.sigmoid(jnp.dot(xcb, gxw_ref[...], preferred_element_type=F32) + gxb_ref[...])
    z = -lam_ref[...]
    softplus = jnp.maximum(z, 0.0) + jnp.log1p(jnp.exp(-jnp.abs(z)))
    log_a = (-LRU_C * r) * softplus
    a = jnp.exp(log_a)
    mult = jnp.sqrt(-jnp.tanh(log_a) * (a * a + 1.0))
    row = lax.broadcasted_iota(jnp.int32, (n, 1), 0)
    mult = jnp.where(row + (pos0 + tt * n) == 0, 1.0, mult)
    u = mult * (ig * xc)

    acc_a, acc_u = a, u
    s = 1
    while s < n:
        keep = row >= s
        prev_u = jnp.where(keep, pltpu.roll(acc_u, s, 0), 0.0)
        prev_a = jnp.where(keep, pltpu.roll(acc_a, s, 0), 1.0)
        acc_u = acc_a * prev_u + acc_u
        acc_a = acc_a * prev_a
        s *= 2
    h = acc_u + acc_a * hcar[...]
    yao_ref[...] = (h * _gelu(ya_ref[...])).astype(BF16)

    cv_in = cg_ref[...] * vv_ref[...]
    cvbuf[8:8 + n, :] = cv_in
    sw = sw_ref[...]
    cv = sw[0:1] * cvbuf[6:6 + n, :]
    cv = cv + sw[1:2] * cvbuf[7:7 + n, :]
    cv = cv + sw[2:3] * cv_in
    ybo_ref[...] = (bg_ref[...] * cv).astype(BF16)

    xbuf[5:8, :] = xa[n - 3:n, :]
    cvbuf[6:8, :] = cv_in[n - 2:n, :]
    hcar[...] = h[n - 1:n, :]

    @pl.when(tt == pl.num_programs(2) - 1)
    def _():
        nconv_ref[...] = xa[n - 3:n, :]
        nh_ref[...] = h[n - 1:n, :]
        nsc_ref[...] = cv_in[n - 2:n, :]


def _mix(l, proj, states, prm, tt_rows, n_streams, n_tiles, row0, pos0, y_prev, width):
    t = proj.shape[0]
    cb = LRU_BLOCK
    nblk = width // cb
    rb0 = row0 // tt_rows
    st_conv, st_h, st_sc = states
    aliased = y_prev is not None

    def col(k):
        return pl.BlockSpec((tt_rows, cb),
                            lambda hb, b, tt: (rb0 + b * n_tiles + tt, k * nblk + hb))

    def vec(rows):
        return pl.BlockSpec((None, rows, cb), lambda hb, b, tt: (l, 0, hb))

    def st(rows):
        return pl.BlockSpec((None, rows, cb), lambda hb, b, tt: (b, 0, hb))

    gate_w = pl.BlockSpec((None, None, cb, cb), lambda hb, b, tt: (l, hb, 0, 0))
    in_specs = [col(0), col(1), col(2), col(3), col(4), st(3), st(1), st(2),
                vec(4), vec(1), gate_w, vec(1), gate_w, vec(1), vec(1), vec(3)]
    args = [proj, proj, proj, proj, proj, st_conv, st_h, st_sc,
            prm['lru_conv_w'], prm['lru_conv_b'], prm['lru_ga_w'], prm['lru_ga_b'],
            prm['lru_gx_w'], prm['lru_gx_b'], prm['lru_lambda'], prm['sconv_w']]
    io_alias = {}
    if aliased:
        in_specs += [pl.BlockSpec(memory_space=pl.ANY)] * 2
        args += list(y_prev)
        io_alias = {16: 0, 17: 1}
    yspec = pl.BlockSpec((tt_rows, cb), lambda hb, b, tt: (rb0 + b * n_tiles + tt, hb))
    out_specs = [yspec, yspec, st(3), st(1), st(2)]
    out_shape = [jax.ShapeDtypeStruct((t, width), BF16),
                 jax.ShapeDtypeStruct((t, width), BF16),
                 jax.ShapeDtypeStruct((n_streams, 3, width), F32),
                 jax.ShapeDtypeStruct((n_streams, 1, width), F32),
                 jax.ShapeDtypeStruct((n_streams, 2, width), F32)]
    return pl.pallas_call(
        functools.partial(_mix_kernel, tt_rows, pos0, aliased),
        grid=(nblk, n_streams, n_tiles),
        in_specs=in_specs, out_specs=out_specs, out_shape=out_shape,
        scratch_shapes=[pltpu.VMEM((SUBLANES + tt_rows, cb), F32),
                        pltpu.VMEM((SUBLANES + tt_rows, cb), F32),
                        pltpu.VMEM((1, cb), F32)],
        input_output_aliases=io_alias,
        compiler_params=_params(("arbitrary", "arbitrary", "arbitrary")),
        name="mix_sample" if aliased else "mix_prompt",
    )(*args)


def _merge_kernel(ya_ref, yb_ref, ga_ref, gb_ref, wa_ref, wb_ref, o_ref):
    pa = jnp.dot(ya_ref[...], wa_ref[...], preferred_element_type=F32)
    pb = jnp.dot(yb_ref[...], wb_ref[...], preferred_element_type=F32)
    o_ref[...] = (jax.nn.sigmoid(ga_ref[...]) * pa + jax.nn.sigmoid(gb_ref[...]) * pb).astype(BF16)


def _merge(l, y_a, y_b, proj, w_a, w_b, gate_col0):
    t, w = y_a.shape
    d = w_a.shape[-1]
    tn = 1024
    nj = d // tn
    gc = gate_col0 // tn
    yspec = pl.BlockSpec((TB, w), lambda i, j: (i, 0))
    wspec = pl.BlockSpec((None, w, tn), lambda i, j: (l, 0, j))
    return pl.pallas_call(
        _merge_kernel,
        grid=(t // TB, nj),
        in_specs=[yspec, yspec,
                  pl.BlockSpec((TB, tn), lambda i, j: (i, gc + j)),
                  pl.BlockSpec((TB, tn), lambda i, j: (i, gc + nj + j)),
                  wspec, wspec],
        out_specs=pl.BlockSpec((TB, tn), lambda i, j: (i, j)),
        out_shape=jax.ShapeDtypeStruct((t, d), BF16),
        compiler_params=_params(("arbitrary", "arbitrary")),
        name="merge",
    )(y_a, y_b, proj, proj, w_a, w_b)


def _outproj_kernel(m_ref, w_ref, x_ref, gt_ref, o_ref):
    y = jnp.dot(m_ref[...], w_ref[...], preferred_element_type=F32)
    o_ref[...] = x_ref[...] + _gate_rows(y, gt_ref[...])


def _outproj(l, merged, w_o, x, mod, n_prompt_tiles):
    t, d = x.shape
    row = pl.BlockSpec((TB, d), lambda i: (i, 0))
    return pl.pallas_call(
        _outproj_kernel,
        grid=(t // TB,),
        in_specs=[row, pl.BlockSpec((None, d, d), lambda i: (l, 0, 0)), row,
                  _mod_spec(l, 2, d, n_prompt_tiles)],
        out_specs=row,
        out_shape=jax.ShapeDtypeStruct((t, d), F32),
        compiler_params=_params(("arbitrary",)),
        name="out_proj",
    )(merged, w_o, x, mod)


def _cmpx(v, i, j):
    hi = jnp.maximum(v[i], v[j])
    lo = jnp.minimum(v[i], v[j])
    v[i], v[j] = hi, lo


def _bitonic_merge_desc(v):
    n = len(v)
    j = n // 2
    while j >= 1:
        for i in range(n):
            p = i ^ j
            if p > i:
                _cmpx(v, i, p)
        j //= 2


def _bitonic_sort_desc(v):
    n = len(v)
    k = 2
    while k <= n:
        j = k // 2
        while j >= 1:
            for i in range(n):
                p = i ^ j
                if p > i:
                    if (i & k) == 0 or k == n:
                        _cmpx(v, i, p)
                    else:
                        _cmpx(v, p, i)
            j //= 2
        k *= 2


def _merge_top(a, b):
    n = len(a)
    c = [jnp.maximum(a[i], b[n - 1 - i]) for i in range(n)]
    _bitonic_merge_desc(c)
    return c


def _top16_of_keys(rows):
    v = list(rows)
    _bitonic_sort_desc(v)
    for shift in (4, 2, 1):
        other = [pltpu.roll(x, shift, 0) for x in v]
        v = _merge_top(v, other)
    return v


def _query_kernel(x_ref, g_ref, sc_ref, sh_ref, mc_ref,
                  h2t_ref, e1_ref, c_ref, e2_ref, s2_ref, st_scr):
    nk = PEER_NKEYS
    k = PEER_TOPK
    h = _rms_mod(x_ref[...], g_ref[...], sc_ref[...], sh_ref[...])
    ht = h.T.astype(BF16)
    h2t_ref[...] = ht
    st_scr[...] = jnp.dot(mc_ref[...], ht, preferred_element_type=F32)
    neg = jnp.full((SUBLANES, LANES), -jnp.inf, F32)

    for lb in range(TB // LANES):
        lanes = slice(lb * LANES, (lb + 1) * LANES)

        def body(hh, carry):
            r1 = pl.multiple_of(hh * nk, nk)
            r2 = pl.multiple_of((PEER_HEADS + hh) * nk, nk)
            s1 = st_scr[pl.ds(r1, nk), lanes]
            s2 = st_scr[pl.ds(r2, nk), lanes]
            s1l = [s1[SUBLANES * i:SUBLANES * (i + 1), :] for i in range(nk // SUBLANES)]
            s2l = [s2[SUBLANES * i:SUBLANES * (i + 1), :] for i in range(nk // SUBLANES)]
            m1 = _top16_of_keys(s1l)
            m2 = _top16_of_keys(s2l)
            g0 = [m1[0] + m2[b] for b in range(k)]
            g1 = [m1[a] + m2[0] for a in range(1, k)] + [neg]
            rest = [m1[a] + m2[b] for a in range(1, k) for b in range(1, k)
                    if (a + 1) * (b + 1) <= k]
            rest = rest + [neg] * (2 * k - len(rest))
            g2, g3 = rest[:k], rest[k:]
            _bitonic_sort_desc(g2)
            _bitonic_sort_desc(g3)
            top = _merge_top(_merge_top(g0, g1), _merge_top(g2, g3))
            thr = top[k - 1]
            zsum = jnp.exp(top[0] - top[0])
            for i in range(1, k):
                zsum = zsum + jnp.exp(top[i] - top[0])
            inv_z = 1.0 / zsum
            e1_ref[pl.ds(r1, nk), lanes] = jnp.concatenate(
                [jnp.exp(x - m1[0]) for x in s1l], axis=0)
            c_ref[pl.ds(r1, nk), lanes] = jnp.concatenate([thr - x for x in s1l], axis=0)
            e2_ref[pl.ds(r1, nk), lanes] = jnp.concatenate(
                [jnp.exp(x - m2[0]) * inv_z for x in s2l], axis=0)
            s2_ref[pl.ds(r1, nk), lanes] = s2
            return carry

        lax.fori_loop(0, PEER_HEADS, body, 0)


def _query(l, x, mod, norm_g, mc, n_prompt_tiles):
    t, d = x.shape
    nq = mc.shape[1]
    rows = PEER_HEADS * PEER_NKEYS
    colblk = pl.BlockSpec((rows, TB), lambda i: (0, i))
    sds = jax.ShapeDtypeStruct((rows, t), F32)
    return pl.pallas_call(
        _query_kernel,
        grid=(t // TB,),
        in_specs=[pl.BlockSpec((TB, d), lambda i: (i, 0)),
                  pl.BlockSpec((None, 1, d), lambda i: (l, 0, 0)),
                  _mod_spec(l, 4, d, n_prompt_tiles),
                  _mod_spec(l, 3, d, n_prompt_tiles),
                  pl.BlockSpec((None, nq, d), lambda i: (l, 0, 0))],
        out_specs=[pl.BlockSpec((d, TB), lambda i: (0, i)), colblk, colblk, colblk, colblk],
        out_shape=[jax.ShapeDtypeStruct((d, t), BF16), sds, sds, sds, sds],
        scratch_shapes=[pltpu.VMEM((nq, TB), F32)],
        compiler_params=_params(("arbitrary",)),
        name="peer_query",
    )(x, norm_g, mod, mod, mc)


def _peer_kernel(n_i, h2t_ref, u_ref, vt_ref, e1_ref, c_ref, e2_ref, s2_ref, po_ref,
                 acc_scr, w_scr, act_scr):
    et = pl.program_id(1)
    nk = PEER_NKEYS

    @pl.when(et == 0)
    def _():
        acc_scr[...] = jnp.zeros_like(acc_scr)

    jrows = 4 * SUBLANES
    half = 2 * LANES

    def stage(hb, carry):
        hl = pl.ds(pl.multiple_of(hb * half, half), half)
        act_scr[:, hl] = jnp.dot(u_ref[...], h2t_ref[:, hl], preferred_element_type=F32)
        for sub in range(half // LANES):
            lanes = pl.ds(pl.multiple_of(hb * half + sub * LANES, LANES), LANES)
            for jc in range(nk // jrows):
                rows = slice(jc * jrows, (jc + 1) * jrows)
                accs = [jnp.zeros((jrows, LANES), F32) for _ in range(n_i)]
                for hh in range(PEER_HEADS):
                    s2v = s2_ref[hh, rows, lanes]
                    e2v = e2_ref[hh, rows, lanes]
                    for il in range(n_i):
                        cr = c_ref[hh, il:il + 1, lanes]
                        er = e1_ref[hh, il:il + 1, lanes]
                        accs[il] = accs[il] + jnp.where(s2v >= cr, e2v * er, 0.0)
                for il in range(n_i):
                    w_scr[il * nk + jc * jrows:il * nk + (jc + 1) * jrows, lanes] = accs[il]
        return carry

    lax.fori_loop(0, TB // half, stage, 0)

    y = (w_scr[...] * _gelu(act_scr[...])).astype(BF16)
    acc_scr[...] += jnp.dot(vt_ref[...], y, preferred_element_type=F32)

    @pl.when(et == pl.num_programs(1) - 1)
    def _():
        po_ref[...] = acc_scr[...].T


def _peer(l, h2t, u, vt, e1, c, e2, s2):
    d, t = h2t.shape
    ne = u.shape[1]
    n_i = 8
    te = n_i * PEER_NKEYS
    small = pl.BlockSpec((PEER_HEADS, n_i, TB), lambda i, e: (0, e, i))
    full = pl.BlockSpec((PEER_HEADS, PEER_NKEYS, TB), lambda i, e: (0, 0, i))
    return pl.pallas_call(
        functools.partial(_peer_kernel, n_i),
        grid=(t // TB, ne // te),
        in_specs=[pl.BlockSpec((d, TB), lambda i, e: (0, i)),
                  pl.BlockSpec((None, te, d), lambda i, e: (l, e, 0)),
                  pl.BlockSpec((None, d, te), lambda i, e: (l, 0, e)),
                  small, small, full, full],
        out_specs=pl.BlockSpec((TB, d), lambda i, e: (i, 0)),
        out_shape=jax.ShapeDtypeStruct((t, d), F32),
        scratch_shapes=[pltpu.VMEM((d, TB), F32), pltpu.VMEM((te, TB), F32),
                        pltpu.VMEM((te, TB), F32)],
        compiler_params=_params(("arbitrary", "arbitrary")),
        name="peer_dense",
    )(h2t, u, vt, e1, c, e2, s2)


def _final_kernel(n_prompt_tiles, x_ref, po_ref, gt_ref, g_ref, yp_ref, ys_ref):
    x = x_ref[...] + _gate_rows(po_ref[...], gt_ref[...])
    ms = jnp.mean(x * x, axis=-1, keepdims=True)
    y = (x * lax.rsqrt(ms + EPS)) * g_ref[...]
    i = pl.program_id(0)

    @pl.when(i < n_prompt_tiles)
    def _():
        yp_ref[...] = y

    @pl.when(i >= n_prompt_tiles)
    def _():
        ys_ref[...] = y


def _final(l, x, po, mod, final_g, n_prompt_tiles):
    t, d = x.shape
    row = pl.BlockSpec((TB, d), lambda i: (i, 0))
    return pl.pallas_call(
        functools.partial(_final_kernel, n_prompt_tiles),
        grid=(t // TB,),
        in_specs=[row, row, _mod_spec(l, 5, d, n_prompt_tiles),
                  pl.BlockSpec((1, d), lambda i: (0, 0))],
        out_specs=[pl.BlockSpec((TB, d), lambda i: (jnp.minimum(i, n_prompt_tiles - 1), 0)),
                   pl.BlockSpec((TB, d), lambda i: (0, 0))],
        out_shape=[jax.ShapeDtypeStruct((n_prompt_tiles * TB, d), F32),
                   jax.ShapeDtypeStruct((TB, d), F32)],
        compiler_params=_params(("arbitrary",)),
        name="final_norm",
    )(x, po, mod, final_g.reshape(1, d))


def kernel(x_prompt, x_sample, state_lru_conv, state_lru_h, state_sconv, c_prompt, c_sample,
           norm1_g, norm2_g, final_g, ada_w, ada_b, w_in, lru_conv_w, lru_conv_b,
           lru_ga_w, lru_ga_b, lru_gx_w, lru_gx_b, lru_lambda, sconv_w,
           w_a_out, w_b_out, w_o, peer_wq, peer_keys, peer_u, peer_v):
    bp, sp, d = x_prompt.shape
    bs, ss, _ = x_sample.shape
    depth = w_in.shape[0]
    width = lru_lambda.shape[-1]
    assert bp == 1 and bs == DEC_BATCH and ss == DEC_SEQ and bs * ss == TB and sp % TB == 0
    n_prompt_tiles = sp // TB

    x = jnp.concatenate([x_prompt.reshape(sp, d), x_sample.reshape(bs * ss, d)], axis=0)
    c16 = jnp.concatenate([jnp.broadcast_to(c_prompt, (MOD_ROWS, d)), c_sample], axis=0)
    mod = _ada(c16, ada_w, ada_b)
    mc = _compose(peer_keys, peer_wq)

    w_in_b = w_in.astype(BF16)
    w_a_b = w_a_out.astype(BF16)
    w_b_b = w_b_out.astype(BF16)
    w_o_b = w_o.astype(BF16)
    u_b = peer_u.astype(BF16)
    vt_b = jnp.swapaxes(peer_v, 1, 2).astype(BF16)
    prm = {
        'lru_conv_w': lru_conv_w, 'lru_conv_b': lru_conv_b.reshape(depth, 1, width),
        'lru_ga_w': lru_ga_w.astype(BF16), 'lru_ga_b': lru_ga_b.reshape(depth, 1, width),
        'lru_gx_w': lru_gx_w.astype(BF16), 'lru_gx_b': lru_gx_b.reshape(depth, 1, width),
        'lru_lambda': lru_lambda.reshape(depth, 1, width), 'sconv_w': sconv_w,
    }
    n1 = norm1_g.reshape(depth, 1, d)
    n2 = norm2_g.reshape(depth, 1, d)
    zero_states = (jnp.zeros((1, 3, width), F32), jnp.zeros((1, 1, width), F32),
                   jnp.zeros((1, 2, width), F32))

    outs_p, outs_s = [], []
    po = None
    for l in range(depth):
        h1, x = _norm1(l, x, po, mod, n1, n_prompt_tiles)
        proj = _inproj(l, h1, w_in_b)
        y_a, y_b, pc, ph, psc = _mix(l, proj, zero_states, prm, TB, 1, n_prompt_tiles,
                                     0, 0, None, width)
        s_states = (state_lru_conv[l], state_lru_h[l].reshape(bs, 1, width), state_sconv[l])
        y_a, y_b, sc_, sh_, ssc_ = _mix(l, proj, s_states, prm, ss, bs, 1,
                                        sp, PAST_LEN, (y_a, y_b), width)
        outs_p.append((pc, ph.reshape(1, width), psc))
        outs_s.append((sc_, sh_.reshape(bs, width), ssc_))
        merged = _merge(l, y_a, y_b, proj, w_a_b, w_b_b, 2 * width + 3 * width)
        x = _outproj(l, merged, w_o_b, x, mod, n_prompt_tiles)
        h2t, e1, c, e2, s2 = _query(l, x, mod, n2, mc, n_prompt_tiles)
        shape3 = (PEER_HEADS, PEER_NKEYS, x.shape[0])
        po = _peer(l, h2t, u_b, vt_b, e1.reshape(shape3), c.reshape(shape3),
                   e2.reshape(shape3), s2.reshape(shape3))
    y_p, y_s = _final(depth - 1, x, po, mod, final_g, n_prompt_tiles)

    return (y_p.reshape(bp, sp, d), y_s.reshape(bs, ss, d),
            jnp.stack([o[0] for o in outs_p]), jnp.stack([o[1] for o in outs_p]),
            jnp.stack([o[2] for o in outs_p]),
            jnp.stack([o[0] for o in outs_s]), jnp.stack([o[1] for o in outs_s]),
            jnp.stack([o[2] for o in outs_s]))
```

```python
import functools
import math

import jax
import jax.numpy as jnp
from jax import lax
from jax.experimental import pallas as pl
from jax.experimental.pallas import tpu as pltpu

EPS = 1e-6
LRU_C = 8.0
LRU_BLOCK = 256
PEER_HEADS = 8
PEER_DHALF = 128
PEER_NKEYS = 128
PEER_TOPK = 16
N_ADA = 6
PAST_LEN = 2048
DEC_BATCH = 8
DEC_SEQ = 64

TB = 512
MOD_ROWS = 8
LANES = 128
SUBLANES = 8
VMEM_LIMIT = 56 * 1024 * 1024

F32 = jnp.float32
BF16 = jnp.bfloat16


def _params(sem):
    return pltpu.CompilerParams(dimension_semantics=sem, vmem_limit_bytes=VMEM_LIMIT)


def _gelu(x):
    c = math.sqrt(2.0 / math.pi)
    return x * (0.5 * (1.0 + jnp.tanh(c * (x + 0.044715 * (x * x * x)))))


def _rms_mod(x, g, sc, sh):
    d = x.shape[-1]
    ms = jnp.mean(x * x, axis=-1, keepdims=True)
    y = (x * lax.rsqrt(ms + EPS)) * g
    y3 = y.reshape(MOD_ROWS, TB // MOD_ROWS, d)
    y3 = y3 * (1.0 + sc)[:, None, :] + sh[:, None, :]
    return y3.reshape(TB, d)


def _gate_rows(v, gt):
    d = v.shape[-1]
    return (v.reshape(MOD_ROWS, TB // MOD_ROWS, d) * gt[:, None, :]).reshape(TB, d)


def _ada_kernel(c_ref, w_ref, b_ref, o_ref):
    c = c_ref[...]
    s = (c * jax.nn.sigmoid(c)).astype(BF16)
    o_ref[...] = jnp.dot(s, w_ref[...].astype(BF16), preferred_element_type=F32) + b_ref[...]


def _ada(c16, ada_w, ada_b):
    depth, d, n = ada_w.shape
    tn = 1024
    return pl.pallas_call(
        _ada_kernel,
        grid=(depth, n // tn),
        in_specs=[pl.BlockSpec((2 * MOD_ROWS, d), lambda l, j: (0, 0)),
                  pl.BlockSpec((None, d, tn), lambda l, j: (l, 0, j)),
                  pl.BlockSpec((None, 1, tn), lambda l, j: (l, 0, j))],
        out_specs=pl.BlockSpec((None, 2 * MOD_ROWS, tn), lambda l, j: (l, 0, j)),
        out_shape=jax.ShapeDtypeStruct((depth, 2 * MOD_ROWS, n), F32),
        compiler_params=_params(("arbitrary", "arbitrary")),
        name="ada_mod",
    )(c16, ada_w, ada_b.reshape(depth, 1, n))


def _compose_kernel(k_ref, wq_ref, o_ref):
    o_ref[...] = lax.dot_general(
        k_ref[...], wq_ref[...], (((1,), (1,)), ((), ())),
        precision=lax.Precision.HIGHEST, preferred_element_type=F32).astype(BF16)


def _compose(peer_keys, peer_wq):
    depth, d, nq = peer_wq.shape
    nb = 2 * PEER_HEADS
    return pl.pallas_call(
        _compose_kernel,
        grid=(depth, nb),
        in_specs=[pl.BlockSpec((None, None, None, PEER_NKEYS, PEER_DHALF),
                               lambda l, b: (l, b // PEER_HEADS, b % PEER_HEADS, 0, 0)),
                  pl.BlockSpec((None, d, PEER_DHALF),
                               lambda l, b: (l, 0, (b % PEER_HEADS) * 2 + b // PEER_HEADS))],
        out_specs=pl.BlockSpec((None, PEER_NKEYS, d), lambda l, b: (l, b, 0)),
        out_shape=jax.ShapeDtypeStruct((depth, nb * PEER_NKEYS, d), BF16),
        compiler_params=_params(("arbitrary", "arbitrary")),
        name="peer_compose",
    )(peer_keys, peer_wq)


def _norm_kernel(has_delta, *refs):
    if has_delta:
        x_ref, po_ref, gt_ref, g_ref, sc_ref, sh_ref, h_ref, xo_ref = refs
    else:
        x_ref, g_ref, sc_ref, sh_ref, h_ref = refs
    x = x_ref[...]
    if has_delta:
        x = x + _gate_rows(po_ref[...], gt_ref[...])
        xo_ref[...] = x
    h_ref[...] = _rms_mod(x, g_ref[...], sc_ref[...], sh_ref[...]).astype(BF16)


def _mod_spec(l, k, d, n_prompt_tiles):
    return pl.BlockSpec((None, MOD_ROWS, d),
                        lambda i, *_: (l, jnp.where(i < n_prompt_tiles, 0, 1), k))


def _norm1(l, x, po, mod, norm_g, n_prompt_tiles):
    t, d = x.shape
    has_delta = po is not None
    row = pl.BlockSpec((TB, d), lambda i: (i, 0))
    in_specs = [row]
    args = [x]
    if has_delta:
        in_specs += [row, _mod_spec(l - 1, 5, d, n_prompt_tiles)]
        args += [po, mod]
    in_specs += [pl.BlockSpec((None, 1, d), lambda i: (l, 0, 0)),
                 _mod_spec(l, 1, d, n_prompt_tiles),
                 _mod_spec(l, 0, d, n_prompt_tiles)]
    args += [norm_g, mod, mod]
    out_specs = [row]
    out_shape = [jax.ShapeDtypeStruct((t, d), BF16)]
    if has_delta:
        out_specs.append(row)
        out_shape.append(jax.ShapeDtypeStruct((t, d), F32))
    res = pl.pallas_call(
        functools.partial(_norm_kernel, has_delta),
        grid=(t // TB,),
        in_specs=in_specs, out_specs=out_specs, out_shape=out_shape,
        compiler_params=_params(("arbitrary",)),
        name="norm1",
    )(*args)
    if has_delta:
        return res[0], res[1]
    return res[0], x


def _inproj_kernel(h_ref, w_ref, o_ref):
    o_ref[...] = jnp.dot(h_ref[...], w_ref[...], preferred_element_type=F32)


def _inproj(l, h, w_in):
    t, d = h.shape
    n = w_in.shape[-1]
    tn = 2048
    return pl.pallas_call(
        _inproj_kernel,
        grid=(n // tn, t // TB),
        in_specs=[pl.BlockSpec((TB, d), lambda j, i: (i, 0)),
                  pl.BlockSpec((None, d, tn), lambda j, i: (l, 0, j))],
        out_specs=pl.BlockSpec((TB, tn), lambda j, i: (i, j)),
        out_shape=jax.ShapeDtypeStruct((t, n), F32),
        compiler_params=_params(("arbitrary", "arbitrary")),
        name="in_proj",
    )(h, w_in)


def _mix_kernel(tt_rows, pos0, aliased, *refs):
    (xa_ref, ya_ref, bg_ref, cg_ref, vv_ref, sconv_ref, sh_ref, ssc_ref,
     cw_ref, cb_ref, gaw_ref, gab_ref, gxw_ref, gxb_ref, lam_ref, sw_ref) = refs[:16]
    refs = refs[16 + (2 if aliased else 0):]
    (yao_ref, ybo_ref, nconv_ref, nh_ref, nsc_ref,
     xbuf, cvbuf, hcar, ascr, uscr, hscr, gin) = refs
    tt = pl.program_id(2)
    n = tt_rows

    @pl.when(tt == 0)
    def _():
        xbuf[5:8, :] = sconv_ref[...]
        cvbuf[6:8, :] = ssc_ref[...]
        hcar[...] = sh_ref[...]

    xa = xa_ref[...]
    xbuf[8:8 + n, :] = xa
    cw = cw_ref[...]
    xc = cw[0:1] * xbuf[5:5 + n, :]
    xc = xc + cw[1:2] * xbuf[6:6 + n, :]
    xc = xc + cw[2:3] * xbuf[7:7 + n, :]
    xc = xc + cw[3:4] * xa
    xc = xc + cb_ref[...]

    xcb = xc.astype(BF16)
    r = jax.nn.sigmoid(jnp.dot(xcb, gaw_ref[...], preferred_element_type=F32) + gab_ref[...])
    ig = jax.nn.sigmoid(jnp.dot(xcb, gxw_ref[...], preferred_element_type=F32) + gxb_ref[...])
    z = -lam_ref[...]
    softplus = jnp.maximum(z, 0.0) + jnp.log1p(jnp.exp(-jnp.abs(z)))
    log_a = (-LRU_C * r) * softplus
    a = jnp.exp(log_a)
    msq = -jnp.tanh(log_a) * (a * a + 1.0)
    mult = jnp.where(msq > 0.0, msq * lax.rsqrt(msq), 0.0)
    row = lax.broadcasted_iota(jnp.int32, (n, 1), 0)
    mult = jnp.where(row + (pos0 + tt * n) == 0, 1.0, mult)
    u = mult * (ig * xc)

    g = n // SUBLANES
    cb = a.shape[-1]
    a3 = a.reshape(g, SUBLANES, cb)
    u3 = u.reshape(g, SUBLANES, cb)
    sub = lax.broadcasted_iota(jnp.int32, (g, SUBLANES, cb), 1)
    s = 1
    while s < SUBLANES:
        keep = sub >= s
        prev_u = jnp.where(keep, pltpu.roll(u3, s, 1), 0.0)
        prev_a = jnp.where(keep, pltpu.roll(a3, s, 1), 1.0)
        u3 = a3 * prev_u + u3
        a3 = a3 * prev_a
        s *= 2
    a2 = a3.reshape(n, cb)
    u2 = u3.reshape(n, cb)
    last = pl.ds(SUBLANES - 1, g, stride=SUBLANES)
    for k in range(cb // LANES):
        ascr[k] = a2[:, k * LANES:(k + 1) * LANES]
        uscr[k] = u2[:, k * LANES:(k + 1) * LANES]
    ga = jnp.concatenate([ascr[k, last, :] for k in range(cb // LANES)], axis=1)
    gu = jnp.concatenate([uscr[k, last, :] for k in range(cb // LANES)], axis=1)
    grow = lax.broadcasted_iota(jnp.int32, (g, 1), 0)
    s = 1
    while s < g:
        keep = grow >= s
        prev_u = jnp.where(keep, pltpu.roll(gu, s, 0), 0.0)
        prev_a = jnp.where(keep, pltpu.roll(ga, s, 0), 1.0)
        gu = ga * prev_u + gu
        ga = ga * prev_a
        s *= 2
    h0 = hcar[...]
    hend = gu + ga * h0
    gin[...] = jnp.where(grow >= 1, pltpu.roll(hend, 1, 0), h0)
    for gi in range(g):
        rows = slice(gi * SUBLANES, (gi + 1) * SUBLANES)
        hscr[rows, :] = u2[rows, :] + a2[rows, :] * gin[gi:gi + 1, :]
    h = hscr[...]
    yao_ref[...] = (h * _gelu(ya_ref[...])).astype(BF16)

    cv_in = cg_ref[...] * vv_ref[...]
    cvbuf[8:8 + n, :] = cv_in
    sw = sw_ref[...]
    cv = sw[0:1] * cvbuf[6:6 + n, :]
    cv = cv + sw[1:2] * cvbuf[7:7 + n, :]
    cv = cv + sw[2:3] * cv_in
    ybo_ref[...] = (bg_ref[...] * cv).astype(BF16)

    xbuf[5:8, :] = xa[n - 3:n, :]
    cvbuf[6:8, :] = cv_in[n - 2:n, :]
    hcar[...] = h[n - 1:n, :]

    @pl.when(tt == pl.num_programs(2) - 1)
    def _():
        nconv_ref[...] = xa[n - 3:n, :]
        nh_ref[...] = h[n - 1:n, :]
        nsc_ref[...] = cv_in[n - 2:n, :]


def _mix(l, proj, states, prm, tt_rows, n_streams, n_tiles, row0, pos0, y_prev, width):
    t = proj.shape[0]
    cb = LRU_BLOCK
    nblk = width // cb
    rb0 = row0 // tt_rows
    st_conv, st_h, st_sc = states
    aliased = y_prev is not None

    def col(k):
        return pl.BlockSpec((tt_rows, cb),
                            lambda hb, b, tt: (rb0 + b * n_tiles + tt, k * nblk + hb))

    def vec(rows):
        return pl.BlockSpec((None, rows, cb), lambda hb, b, tt: (l, 0, hb))

    def st(rows):
        return pl.BlockSpec((None, rows, cb), lambda hb, b, tt: (b, 0, hb))

    gate_w = pl.BlockSpec((None, None, cb, cb), lambda hb, b, tt: (l, hb, 0, 0))
    in_specs = [col(0), col(1), col(2), col(3), col(4), st(3), st(1), st(2),
                vec(4), vec(1), gate_w, vec(1), gate_w, vec(1), vec(1), vec(3)]
    args = [proj, proj, proj, proj, proj, st_conv, st_h, st_sc,
            prm['lru_conv_w'], prm['lru_conv_b'], prm['lru_ga_w'], prm['lru_ga_b'],
            prm['lru_gx_w'], prm['lru_gx_b'], prm['lru_lambda'], prm['sconv_w']]
    io_alias = {}
    if aliased:
        in_specs += [pl.BlockSpec(memory_space=pl.ANY)] * 2
        args += list(y_prev)
        io_alias = {16: 0, 17: 1}
    yspec = pl.BlockSpec((tt_rows, cb), lambda hb, b, tt: (rb0 + b * n_tiles + tt, hb))
    out_specs = [yspec, yspec, st(3), st(1), st(2)]
    out_shape = [jax.ShapeDtypeStruct((t, width), BF16),
                 jax.ShapeDtypeStruct((t, width), BF16),
                 jax.ShapeDtypeStruct((n_streams, 3, width), F32),
                 jax.ShapeDtypeStruct((n_streams, 1, width), F32),
                 jax.ShapeDtypeStruct((n_streams, 2, width), F32)]
    return pl.pallas_call(
        functools.partial(_mix_kernel, tt_rows, pos0, aliased),
        grid=(nblk, n_streams, n_tiles),
        in_specs=in_specs, out_specs=out_specs, out_shape=out_shape,
        scratch_shapes=[pltpu.VMEM((SUBLANES + tt_rows, cb), F32),
                        pltpu.VMEM((SUBLANES + tt_rows, cb), F32),
                        pltpu.VMEM((1, cb), F32),
                        pltpu.VMEM((cb // LANES, tt_rows, LANES), F32),
                        pltpu.VMEM((cb // LANES, tt_rows, LANES), F32),
                        pltpu.VMEM((tt_rows, cb), F32),
                        pltpu.VMEM((tt_rows // SUBLANES, cb), F32)],
        input_output_aliases=io_alias,
        compiler_params=_params(("arbitrary", "arbitrary", "arbitrary")),
        name="mix_sample" if aliased else "mix_prompt",
    )(*args)


def _merge_kernel(ya_ref, yb_ref, ga_ref, gb_ref, wa_ref, wb_ref, o_ref):
    pa = jnp.dot(ya_ref[...], wa_ref[...], preferred_element_type=F32)
    pb = jnp.dot(yb_ref[...], wb_ref[...], preferred_element_type=F32)
    o_ref[...] = (jax.nn.sigmoid(ga_ref[...]) * pa + jax.nn.sigmoid(gb_ref[...]) * pb).astype(BF16)


def _merge(l, y_a, y_b, proj, w_a, w_b, gate_col0):
    t, w = y_a.shape
    d = w_a.shape[-1]
    tn = 1024
    nj = d // tn
    gc = gate_col0 // tn
    yspec = pl.BlockSpec((TB, w), lambda i, j: (i, 0))
    wspec = pl.BlockSpec((None, w, tn), lambda i, j: (l, 0, j))
    return pl.pallas_call(
        _merge_kernel,
        grid=(t // TB, nj),
        in_specs=[yspec, yspec,
                  pl.BlockSpec((TB, tn), lambda i, j: (i, gc + j)),
                  pl.BlockSpec((TB, tn), lambda i, j: (i, gc + nj + j)),
                  wspec, wspec],
        out_specs=pl.BlockSpec((TB, tn), lambda i, j: (i, j)),
        out_shape=jax.ShapeDtypeStruct((t, d), BF16),
        compiler_params=_params(("arbitrary", "arbitrary")),
        name="merge",
    )(y_a, y_b, proj, proj, w_a, w_b)


def _outproj_kernel(m_ref, w_ref, x_ref, gt_ref, o_ref):
    y = jnp.dot(m_ref[...], w_ref[...], preferred_element_type=F32)
    o_ref[...] = x_ref[...] + _gate_rows(y, gt_ref[...])


def _outproj(l, merged, w_o, x, mod, n_prompt_tiles):
    t, d = x.shape
    row = pl.BlockSpec((TB, d), lambda i: (i, 0))
    return pl.pallas_call(
        _outproj_kernel,
        grid=(t // TB,),
        in_specs=[row, pl.BlockSpec((None, d, d), lambda i: (l, 0, 0)), row,
                  _mod_spec(l, 2, d, n_prompt_tiles)],
        out_specs=row,
        out_shape=jax.ShapeDtypeStruct((t, d), F32),
        compiler_params=_params(("arbitrary",)),
        name="out_proj",
    )(merged, w_o, x, mod)


def _cmpx(v, i, j):
    hi = jnp.maximum(v[i], v[j])
    lo = jnp.minimum(v[i], v[j])
    v[i], v[j] = hi, lo


def _bitonic_merge_desc(v):
    n = len(v)
    j = n // 2
    while j >= 1:
        for i in range(n):
            p = i ^ j
            if p > i:
                _cmpx(v, i, p)
        j //= 2


def _bitonic_sort_desc(v):
    n = len(v)
    k = 2
    while k <= n:
        j = k // 2
        while j >= 1:
            for i in range(n):
                p = i ^ j
                if p > i:
                    if (i & k) == 0 or k == n:
                        _cmpx(v, i, p)
                    else:
                        _cmpx(v, p, i)
            j //= 2
        k *= 2


def _merge_top(a, b):
    n = len(a)
    c = [jnp.maximum(a[i], b[n - 1 - i]) for i in range(n)]
    _bitonic_merge_desc(c)
    return c


def _merge_top_lost(a, b, lost):
    n = len(a)
    for i in range(n):
        lost = jnp.maximum(lost, jnp.minimum(a[i], b[n - 1 - i]))
    return _merge_top(a, b), lost


def _top16_of_keys(rows):
    v = list(rows)
    _bitonic_sort_desc(v)
    for shift in (4, 2, 1):
        other = [pltpu.roll(x, shift, 0) for x in v]
        v = _merge_top(v, other)
    return v


def _query_kernel(x_ref, g_ref, sc_ref, sh_ref, mc_ref,
                  h2t_ref, e1_ref, c_ref, e2_ref, s2_ref, st_scr):
    nk = PEER_NKEYS
    k = PEER_TOPK
    h = _rms_mod(x_ref[...], g_ref[...], sc_ref[...], sh_ref[...])
    ht = h.T.astype(BF16)
    h2t_ref[...] = ht
    st_scr[...] = jnp.dot(mc_ref[...], ht, preferred_element_type=F32)
    neg = jnp.full((SUBLANES, LANES), -jnp.inf, F32)

    for lb in range(TB // LANES):
        lanes = slice(lb * LANES, (lb + 1) * LANES)

        def body(hh, carry):
            r1 = pl.multiple_of(hh * nk, nk)
            rp = pl.multiple_of((PEER_HEADS + hh) * nk, nk)
            s1 = st_scr[pl.ds(r1, nk), lanes]
            s2 = st_scr[pl.ds(rp, nk), lanes]
            s1l = [s1[SUBLANES * i:SUBLANES * (i + 1), :] for i in range(nk // SUBLANES)]
            s2l = [s2[SUBLANES * i:SUBLANES * (i + 1), :] for i in range(nk // SUBLANES)]
            m1 = _top16_of_keys(s1l)
            m2 = _top16_of_keys(s2l)
            g0 = [m1[0] + m2[b] for b in range(k)]
            g1 = [m1[a] + m2[0] for a in range(1, k)] + [neg]
            rest = [m1[a] + m2[b] for a in range(1, k) for b in range(1, k)
                    if (a + 1) * (b + 1) <= k]
            rest = rest + [neg] * (2 * k - len(rest))
            g2, g3 = rest[:k], rest[k:]
            _bitonic_sort_desc(g2)
            _bitonic_sort_desc(g3)
            t01, lost = _merge_top_lost(g0, g1, neg)
            t23, lost = _merge_top_lost(g2, g3, lost)
            top, lost = _merge_top_lost(t01, t23, lost)
            thr = 0.5 * top[k - 1] + 0.5 * lost
            zsum = jnp.exp(top[0] - top[0])
            for i in range(1, k):
                zsum = zsum + jnp.exp(top[i] - top[0])
            inv_z = 1.0 / zsum
            e1_ref[pl.ds(r1, nk), lanes] = jnp.concatenate(
                [jnp.exp(x - m1[0]) for x in s1l], axis=0)
            c_ref[pl.ds(r1, nk), lanes] = jnp.concatenate([thr - x for x in s1l], axis=0)
            e2_ref[pl.ds(r1, nk), lanes] = jnp.concatenate(
                [jnp.exp(x - m2[0]) * inv_z for x in s2l], axis=0)
            s2_ref[pl.ds(r1, nk), lanes] = s2
            return carry

        lax.fori_loop(0, PEER_HEADS, body, 0)


def _query(l, x, mod, norm_g, mc, n_prompt_tiles):
    t, d = x.shape
    nq = mc.shape[1]
    rows = PEER_HEADS * PEER_NKEYS
    colblk = pl.BlockSpec((rows, TB), lambda i: (0, i))
    sds = jax.ShapeDtypeStruct((rows, t), F32)
    return pl.pallas_call(
        _query_kernel,
        grid=(t // TB,),
        in_specs=[pl.BlockSpec((TB, d), lambda i: (i, 0)),
                  pl.BlockSpec((None, 1, d), lambda i: (l, 0, 0)),
                  _mod_spec(l, 4, d, n_prompt_tiles),
                  _mod_spec(l, 3, d, n_prompt_tiles),
                  pl.BlockSpec((None, nq, d), lambda i: (l, 0, 0))],
        out_specs=[pl.BlockSpec((d, TB), lambda i: (0, i)), colblk, colblk, colblk, colblk],
        out_shape=[jax.ShapeDtypeStruct((d, t), BF16), sds, sds, sds, sds],
        scratch_shapes=[pltpu.VMEM((nq, TB), F32)],
        compiler_params=_params(("arbitrary",)),
        name="peer_query",
    )(x, norm_g, mod, mod, mc)


def _peer_kernel(n_i, h2t_ref, u_ref, vt_ref, e1_ref, c_ref, e2_ref, s2_ref, po_ref,
                 acc_scr, w_scr, act_scr):
    et = pl.program_id(1)
    nk = PEER_NKEYS

    @pl.when(et == 0)
    def _():
        acc_scr[...] = jnp.zeros_like(acc_scr)

    jrows = 4 * SUBLANES
    half = 2 * LANES

    def stage(hb, carry):
        hl = pl.ds(pl.multiple_of(hb * half, half), half)
        act_scr[:, hl] = jnp.dot(u_ref[...], h2t_ref[:, hl], preferred_element_type=F32)
        for sub in range(half // LANES):
            lanes = pl.ds(pl.multiple_of(hb * half + sub * LANES, LANES), LANES)
            for jc in range(nk // jrows):
                rows = slice(jc * jrows, (jc + 1) * jrows)
                accs = [jnp.zeros((jrows, LANES), F32) for _ in range(n_i)]
                for hh in range(PEER_HEADS):
                    s2v = s2_ref[hh, rows, lanes]
                    e2v = e2_ref[hh, rows, lanes]
                    for il in range(n_i):
                        cr = c_ref[hh, il:il + 1, lanes]
                        er = e1_ref[hh, il:il + 1, lanes]
                        accs[il] = accs[il] + jnp.where(s2v >= cr, e2v * er, 0.0)
                for il in range(n_i):
                    w_scr[il * nk + jc * jrows:il * nk + (jc + 1) * jrows, lanes] = accs[il]
        return carry

    lax.fori_loop(0, TB // half, stage, 0)

    y = (w_scr[...] * _gelu(act_scr[...])).astype(BF16)
    acc_scr[...] += jnp.dot(vt_ref[...], y, preferred_element_type=F32)

    @pl.when(et == pl.num_programs(1) - 1)
    def _():
        po_ref[...] = acc_scr[...].T


def _peer(l, h2t, u, vt, e1, c, e2, s2):
    d, t = h2t.shape
    ne = u.shape[1]
    n_i = 8
    te = n_i * PEER_NKEYS
    small = pl.BlockSpec((PEER_HEADS, n_i, TB), lambda i, e: (0, e, i))
    full = pl.BlockSpec((PEER_HEADS, PEER_NKEYS, TB), lambda i, e: (0, 0, i))
    return pl.pallas_call(
        functools.partial(_peer_kernel, n_i),
        grid=(t // TB, ne // te),
        in_specs=[pl.BlockSpec((d, TB), lambda i, e: (0, i)),
                  pl.BlockSpec((None, te, d), lambda i, e: (l, e, 0)),
                  pl.BlockSpec((None, d, te), lambda i, e: (l, 0, e)),
                  small, small, full, full],
        out_specs=pl.BlockSpec((TB, d), lambda i, e: (i, 0)),
        out_shape=jax.ShapeDtypeStruct((t, d), F32),
        scratch_shapes=[pltpu.VMEM((d, TB), F32), pltpu.VMEM((te, TB), F32),
                        pltpu.VMEM((te, TB), F32)],
        compiler_params=_params(("arbitrary", "arbitrary")),
        name="peer_dense",
    )(h2t, u, vt, e1, c, e2, s2)


def _final_kernel(n_prompt_tiles, x_ref, po_ref, gt_ref, g_ref, yp_ref, ys_ref):
    x = x_ref[...] + _gate_rows(po_ref[...], gt_ref[...])
    ms = jnp.mean(x * x, axis=-1, keepdims=True)
    y = (x * lax.rsqrt(ms + EPS)) * g_ref[...]
    i = pl.program_id(0)

    @pl.when(i < n_prompt_tiles)
    def _():
        yp_ref[...] = y

    @pl.when(i >= n_prompt_tiles)
    def _():
        ys_ref[...] = y


def _final(l, x, po, mod, final_g, n_prompt_tiles):
    t, d = x.shape
    row = pl.BlockSpec((TB, d), lambda i: (i, 0))
    return pl.pallas_call(
        functools.partial(_final_kernel, n_prompt_tiles),
        grid=(t // TB,),
        in_specs=[row, row, _mod_spec(l, 5, d, n_prompt_tiles),
                  pl.BlockSpec((1, d), lambda i: (0, 0))],
        out_specs=[pl.BlockSpec((TB, d), lambda i: (jnp.minimum(i, n_prompt_tiles - 1), 0)),
                   pl.BlockSpec((TB, d), lambda i: (0, 0))],
        out_shape=[jax.ShapeDtypeStruct((n_prompt_tiles * TB, d), F32),
                   jax.ShapeDtypeStruct((TB, d), F32)],
        compiler_params=_params(("arbitrary",)),
        name="final_norm",
    )(x, po, mod, final_g.reshape(1, d))


def kernel(x_prompt, x_sample, state_lru_conv, state_lru_h, state_sconv, c_prompt, c_sample,
           norm1_g, norm2_g, final_g, ada_w, ada_b, w_in, lru_conv_w, lru_conv_b,
           lru_ga_w, lru_ga_b, lru_gx_w, lru_gx_b, lru_lambda, sconv_w,
           w_a_out, w_b_out, w_o, peer_wq, peer_keys, peer_u, peer_v):
    bp, sp, d = x_prompt.shape
    bs, ss, _ = x_sample.shape
    depth = w_in.shape[0]
    width = lru_lambda.shape[-1]
    assert bp == 1 and bs == DEC_BATCH and ss == DEC_SEQ and bs * ss == TB and sp % TB == 0
    n_prompt_tiles = sp // TB

    x = jnp.concatenate([x_prompt.reshape(sp, d), x_sample.reshape(bs * ss, d)], axis=0)
    c16 = jnp.concatenate([jnp.broadcast_to(c_prompt, (MOD_ROWS, d)), c_sample], axis=0)
    mod = _ada(c16, ada_w, ada_b)
    mc = _compose(peer_keys, peer_wq)

    w_in_b = w_in.astype(BF16)
    w_a_b = w_a_out.astype(BF16)
    w_b_b = w_b_out.astype(BF16)
    w_o_b = w_o.astype(BF16)
    u_b = peer_u.astype(BF16)
    vt_b = jnp.swapaxes(peer_v, 1, 2).astype(BF16)
    prm = {
        'lru_conv_w': lru_conv_w, 'lru_conv_b': lru_conv_b.reshape(depth, 1, width),
        'lru_ga_w': lru_ga_w.astype(BF16), 'lru_ga_b': lru_ga_b.reshape(depth, 1, width),
        'lru_gx_w': lru_gx_w.astype(BF16), 'lru_gx_b': lru_gx_b.reshape(depth, 1, width),
        'lru_lambda': lru_lambda.reshape(depth, 1, width), 'sconv_w': sconv_w,
    }
    n1 = norm1_g.reshape(depth, 1, d)
    n2 = norm2_g.reshape(depth, 1, d)
    zero_states = (jnp.zeros((1, 3, width), F32), jnp.zeros((1, 1, width), F32),
                   jnp.zeros((1, 2, width), F32))

    outs_p, outs_s = [], []
    po = None
    for l in range(depth):
        h1, x = _norm1(l, x, po, mod, n1, n_prompt_tiles)
        proj = _inproj(l, h1, w_in_b)
        y_a, y_b, pc, ph, psc = _mix(l, proj, zero_states, prm, TB, 1, n_prompt_tiles,
                                     0, 0, None, width)
        s_states = (state_lru_conv[l], state_lru_h[l].reshape(bs, 1, width), state_sconv[l])
        y_a, y_b, sc_, sh_, ssc_ = _mix(l, proj, s_states, prm, ss, bs, 1,
                                        sp, PAST_LEN, (y_a, y_b), width)
        outs_p.append((pc, ph.reshape(1, width), psc))
        outs_s.append((sc_, sh_.reshape(bs, width), ssc_))
        merged = _merge(l, y_a, y_b, proj, w_a_b, w_b_b, 2 * width + 3 * width)
        x = _outproj(l, merged, w_o_b, x, mod, n_prompt_tiles)
        h2t, e1, c, e2, s2 = _query(l, x, mod, n2, mc, n_prompt_tiles)
        shape3 = (PEER_HEADS, PEER_NKEYS, x.shape[0])
        po = _peer(l, h2t, u_b, vt_b, e1.reshape(shape3), c.reshape(shape3),
                   e2.reshape(shape3), s2.reshape(shape3))
    y_p, y_s = _final(depth - 1, x, po, mod, final_g, n_prompt_tiles)

    return (y_p.reshape(bp, sp, d), y_s.reshape(bs, ss, d),
            jnp.stack([o[0] for o in outs_p]), jnp.stack([o[1] for o in outs_p]),
            jnp.stack([o[2] for o in outs_p]),
            jnp.stack([o[0] for o in outs_s]), jnp.stack([o[1] for o in outs_s]),
            jnp.stack([o[2] for o in outs_s]))
```
